```python
import jax, jax.numpy as jnp
from jax import lax
import numpy as np

D_MODEL = 1024
BATCH = 8
SEQ = 2048
DEPTH = 1
DEC_BATCH = 128
DEC_SEQ = 8
PAST_LEN = 16384
PAGE_SIZE = 128

D_MIX = D_MODEL
D_A = D_MIX // 2
D_B = D_MIX - D_A
H_A = 4
DK_A = D_A // H_A
DV_A = D_A // H_A
H_B = 4
DH_B = D_B // H_B
CMLP_CHUNK = 128
GLA_CHUNK = 64
D_FF = -(-8 * D_MODEL // (3 * 256)) * 256
N_IN = 4 * D_A + 2 * D_B
EPS = 1e-6

kernel_name = "hymba_hgrn2_chunkmlp_decoder_step"


def rmsnorm(x, g):
    xf = x.astype(jnp.float32)
    y = xf * lax.rsqrt(jnp.mean(xf * xf, axis=-1, keepdims=True) + EPS)
    return (y * g.astype(jnp.float32)).astype(x.dtype)


def layernorm(x, g, b):
    xf = x.astype(jnp.float32)
    mu = jnp.mean(xf, axis=-1, keepdims=True)
    xc = xf - mu
    y = xc * lax.rsqrt(jnp.mean(xc * xc, axis=-1, keepdims=True) + EPS)
    return (y * g.astype(jnp.float32) + b.astype(jnp.float32)).astype(x.dtype)


def hgrn2_recurrence(q, logf, k, v, s0):
    B, L = q.shape[0], q.shape[1]
    C = GLA_CHUNK
    N = -(-L // C)
    pad = N * C - L
    padw = ((0, 0), (0, pad), (0, 0), (0, 0))
    q, logf, k, v = [jnp.pad(a.astype(jnp.float32), padw).reshape((B, N, C) + a.shape[2:])
                     for a in (q, logf, k, v)]
    b = jnp.cumsum(logf, axis=2)
    b_last = b[:, :, -1:]
    qd = q * jnp.exp(b)
    kd = k * jnp.exp(-b)
    causal = jnp.tril(jnp.ones((C, C), dtype=bool))
    scores = jnp.einsum('bnthk,bnshk->bnhts', qd, kd)
    scores = jnp.where(causal, scores, 0.0)
    o_intra = jnp.einsum('bnhts,bnshv->bnthv', scores, v)
    k_end = k * jnp.exp(b_last - b)
    ds = jnp.einsum('bnshk,bnshv->bnhkv', k_end, v)
    decay = jnp.exp(b_last[:, :, 0])

    def step(s, inp):
        d, dsn = inp
        return d[..., None] * s + dsn, s

    s_final, s_start = lax.scan(step, s0, (jnp.moveaxis(decay, 1, 0), jnp.moveaxis(ds, 1, 0)))
    s_start = jnp.moveaxis(s_start, 0, 1)
    o_inter = jnp.einsum('bnthk,bnhkv->bnthv', qd, s_start)
    o = (o_intra + o_inter).reshape(B, N * C, q.shape[3], v.shape[4])[:, :L]
    return o, s_final


def hgrn2_mixer(zq, zf, zi, zg, lb, gnorm_w, s0):
    B, L, _ = zq.shape
    q = zq.reshape(B, L, H_A, DK_A)
    lbh = lb.reshape(H_A, DK_A)
    f = lbh + (1.0 - lbh) * jax.nn.sigmoid(zf.reshape(B, L, H_A, DK_A).astype(jnp.float32))
    logf = jnp.log(f)
    k = 1.0 - f
    v = zi.reshape(B, L, H_A, DV_A)
    o, s_new = hgrn2_recurrence(q, logf, k, v, s0.astype(jnp.float32))
    o = rmsnorm(o, gnorm_w) * jax.nn.silu(zg.reshape(B, L, H_A, DV_A).astype(jnp.float32))
    return o.reshape(B, L, D_A).astype(zq.dtype), s_new


def chunk_mlp_mixer(zu, zv, ln_g, ln_b, w_s, b_s):
    B, L, _ = zu.shape
    C = CMLP_CHUNK
    u = jax.nn.gelu(zu).reshape(B, L, H_B, DH_B)
    v = layernorm(jax.nn.gelu(zv), ln_g, ln_b).reshape(B, L, H_B, DH_B)
    N = -(-L // C)
    pad = N * C - L
    vp = jnp.pad(v, ((0, 0), (0, pad), (0, 0), (0, 0))).reshape(B, N, C, H_B, DH_B)
    ws = jnp.where(jnp.tril(jnp.ones((C, C), dtype=bool)), w_s, 0.0)
    mixed = jnp.einsum('hts,bnshd->bnthd', ws, vp) + jnp.transpose(b_s)[None, None, :, :, None]
    mixed = mixed.reshape(B, N * C, H_B, DH_B)[:, :L]
    out = (u * mixed).reshape(B, L, D_B)
    v_rows = v[:, ((L - 1) // C) * C:]
    return out, v_rows


def decoder_layer(x, c, s0, lb, w_ada, b_ada, norm_pre_mix, norm_post_mix, w_in, gnorm_w,
                  ln_v_g, ln_v_b, w_spatial, b_spatial, w_out, norm_pre_ffn, norm_post_ffn,
                  w_gate, w_up, w_down):
    mod = (jax.nn.silu(c) @ w_ada + b_ada)[:, None, :]
    sh1, sc1, g1, sh2, sc2, g2 = jnp.split(mod, 6, axis=-1)
    h = rmsnorm(x, norm_pre_mix) * (1.0 + sc1) + sh1
    z = h @ w_in
    zq, zf, zi, zg, zu, zv = jnp.split(
        z, [D_A, 2 * D_A, 3 * D_A, 4 * D_A, 4 * D_A + D_B], axis=-1)
    oa, s_new = hgrn2_mixer(zq, zf, zi, zg, lb, gnorm_w, s0)
    ob, v_rows = chunk_mlp_mixer(zu, zv, ln_v_g, ln_v_b, w_spatial, b_spatial)
    m = jnp.concatenate([oa, ob], axis=-1) @ w_out
    x = x + g1 * rmsnorm(m, norm_post_mix)
    h2 = rmsnorm(x, norm_pre_ffn) * (1.0 + sc2) + sh2
    f = (jax.nn.silu(h2 @ w_gate) * (h2 @ w_up)) @ w_down
    x = x + g2 * rmsnorm(f, norm_post_ffn)
    return x, s_new, v_rows


def setup_inputs(seed: int = 0) -> dict:
    key = jax.random.key(seed)
    ks = jax.random.split(key, 24)

    def nrm(k, shape, s):
        return s * jax.random.normal(k, shape, jnp.float32)

    return {
        "x_prompt": nrm(ks[0], (BATCH, SEQ, D_MODEL), 1.0),
        "x_sample": nrm(ks[1], (DEC_BATCH, DEC_SEQ, D_MODEL), 1.0),
        "state_hgrn": nrm(ks[2], (DEPTH, DEC_BATCH, H_A, DK_A, DV_A), 0.5),
        "c_prompt": nrm(ks[3], (BATCH, D_MODEL), 1.0),
        "c_sample": nrm(ks[4], (DEC_BATCH, D_MODEL), 1.0),
        "lb_logits": nrm(ks[5], (DEPTH + 1, D_A), 0.1),
        "w_ada": nrm(ks[6], (DEPTH, D_MODEL, 6 * D_MODEL), 0.3 * D_MODEL ** -0.5),
        "b_ada": nrm(ks[7], (DEPTH, 6 * D_MODEL), 0.1),
        "norm_pre_mix": 1.0 + nrm(ks[8], (DEPTH, D_MODEL), 0.1),
        "norm_post_mix": 1.0 + nrm(ks[9], (DEPTH, D_MODEL), 0.1),
        "w_in": nrm(ks[10], (DEPTH, D_MODEL, N_IN), D_MODEL ** -0.5),
        "gnorm_w": 1.0 + nrm(ks[11], (DEPTH, DV_A), 0.1),
        "ln_v_g": 1.0 + nrm(ks[12], (DEPTH, D_B), 0.1),
        "ln_v_b": nrm(ks[13], (DEPTH, D_B), 0.1),
        "w_spatial": nrm(ks[14], (DEPTH, H_B, CMLP_CHUNK, CMLP_CHUNK), CMLP_CHUNK ** -0.5),
        "b_spatial": 1.0 + nrm(ks[15], (DEPTH, H_B, CMLP_CHUNK), 0.1),
        "w_out": nrm(ks[16], (DEPTH, D_MIX, D_MODEL), D_MIX ** -0.5),
        "norm_pre_ffn": 1.0 + nrm(ks[17], (DEPTH, D_MODEL), 0.1),
        "norm_post_ffn": 1.0 + nrm(ks[18], (DEPTH, D_MODEL), 0.1),
        "w_gate": nrm(ks[19], (DEPTH, D_MODEL, D_FF), D_MODEL ** -0.5),
        "w_up": nrm(ks[20], (DEPTH, D_MODEL, D_FF), D_MODEL ** -0.5),
        "w_down": nrm(ks[21], (DEPTH, D_FF, D_MODEL), D_FF ** -0.5),
    }


def reference(x_prompt, x_sample, state_hgrn, c_prompt, c_sample, lb_logits, w_ada, b_ada,
              norm_pre_mix, norm_post_mix, w_in, gnorm_w, ln_v_g, ln_v_b, w_spatial, b_spatial,
              w_out, norm_pre_ffn, norm_post_ffn, w_gate, w_up, w_down):
    lb_all = jnp.cumsum(jax.nn.softmax(lb_logits.astype(jnp.float32), axis=0), axis=0)
    xp, xs = x_prompt, x_sample
    s_p_list, s_s_list, v_p_list, v_s_list = [], [], [], []
    for l in range(DEPTH):
        params = (w_ada[l], b_ada[l], norm_pre_mix[l], norm_post_mix[l], w_in[l], gnorm_w[l],
                  ln_v_g[l], ln_v_b[l], w_spatial[l], b_spatial[l], w_out[l], norm_pre_ffn[l],
                  norm_post_ffn[l], w_gate[l], w_up[l], w_down[l])
        s0_prompt = jnp.zeros((xp.shape[0], H_A, DK_A, DV_A), jnp.float32)
        xp, sp, vp = decoder_layer(xp, c_prompt, s0_prompt, lb_all[l], *params)
        xs, ss, vs = decoder_layer(xs, c_sample, state_hgrn[l], lb_all[l], *params)
        s_p_list.append(sp.astype(x_prompt.dtype))
        s_s_list.append(ss.astype(state_hgrn.dtype))
        v_p_list.append(vp)
        v_s_list.append(vs)
    state_hgrn_prompt = jnp.stack(s_p_list)
    state_hgrn_sample = jnp.stack(s_s_list)
    state_cmlp_v_prompt = jnp.stack(v_p_list)
    state_cmlp_v_sample = jnp.stack(v_s_list)
    return (xp, xs, state_hgrn_prompt, state_hgrn_sample, state_cmlp_v_prompt, state_cmlp_v_sample)
```

```python
import functools

import jax
import jax.numpy as jnp
from jax import lax
from jax.experimental import pallas as pl
from jax.experimental.pallas import tpu as pltpu

EPS = 1e-6
GLA_CHUNK = 64
ROW_BLOCK = 128
V7X_VMEM_LIMIT_BYTES = 56 * 1024 * 1024
F32 = jnp.float32
BF16 = jnp.bfloat16


def _bf(a):
    return a.astype(BF16)


def _dot(a, b):
    return jnp.dot(a, b, preferred_element_type=F32)


def _dot_nt(a, b):
    return lax.dot_general(a, b, (((1,), (1,)), ((), ())), preferred_element_type=F32)


def _rms(x, g):
    return x * lax.rsqrt(jnp.mean(x * x, axis=-1, keepdims=True) + EPS) * g


def _split3(x):
    hi = _bf(x)
    r1 = x - hi.astype(F32)
    mid = _bf(r1)
    lo = _bf(r1 - mid.astype(F32))
    return jnp.concatenate([hi, mid, lo], axis=-1)


def _sum3(y, n):
    return y[:, :n] + y[:, n:2 * n] + y[:, 2 * n:]


def _seg_masks(seg):
    r = lax.broadcasted_iota(jnp.int32, (ROW_BLOCK, ROW_BLOCK), 0)
    c = lax.broadcasted_iota(jnp.int32, (ROW_BLOCK, ROW_BLOCK), 1)
    same = (r // seg) == (c // seg)
    return same, same & (c <= r)


def _forget_lower_bound(lbl, layer):
    e = jnp.exp(lbl - jnp.max(lbl, axis=0, keepdims=True))
    return jnp.sum(e[:layer + 1], axis=0, keepdims=True) / jnp.sum(e, axis=0, keepdims=True)


def _project_in(x3, mod, npre, win_ref, d_a, d_b):
    sh, sc = mod[:, 0:1, :], mod[:, 1:2, :]
    h3 = _rms(x3, npre) * (1.0 + sc) + sh
    hb = _bf(h3.reshape(x3.shape[0] * x3.shape[1], x3.shape[2]))
    edges = [0, d_a, 2 * d_a, 3 * d_a, 4 * d_a, 4 * d_a + d_b, 4 * d_a + 2 * d_b]
    return [_dot(hb, win_ref[:, a:b]) for a, b in zip(edges[:-1], edges[1:])]


def _gla_block_terms(logf, q, k, seg):
    d_a = logf.shape[-1]
    same, causal = _seg_masks(seg)
    scan = _bf(jnp.concatenate([jnp.where(causal, 1.0, 0.0), jnp.where(same, 1.0, 0.0)], axis=0))
    bc = _dot(scan, _split3(logf))
    b = _sum3(bc[:ROW_BLOCK], d_a)
    b_last = _sum3(bc[ROW_BLOCK:], d_a)
    qd = q * jnp.exp(b)
    kd = k * jnp.exp(-b)
    k_end = k * jnp.exp(b_last - b)
    return qd, kd, k_end, b_last, causal


def _head_out(o, zg_h, gn):
    return _rms(o, gn) * jax.nn.silu(zg_h)


def _layernorm(x, g, b):
    mu = jnp.mean(x, axis=-1, keepdims=True)
    xc = x - mu
    return xc * lax.rsqrt(jnp.mean(xc * xc, axis=-1, keepdims=True) + EPS) * g + b


def _spatial_gate_block(u, vv, wmix_ref, bias_ref, m_scr, rows, d_a, n_groups):
    dh = u.shape[-1] // n_groups
    for g in range(n_groups):
        cs = slice(g * dh, (g + 1) * dh)
        mixed = _dot(wmix_ref[g], _bf(vv[:, cs])) + bias_ref[:, cs]
        m_scr[rows, d_a + g * dh:d_a + (g + 1) * dh] = _bf(u[:, cs] * mixed)


def _mixer_prompt_kernel(x_ref, mod_ref, lbl_ref, npre_ref, npost_ref, win_ref, gn_ref, lng_ref,
                         lnb_ref, wmix_ref, bias_ref, wout_ref, y_ref, st_ref, vrow_ref,
                         st_scr, m_scr, *, layer, n_heads, n_groups):
    li = pl.program_id(1)
    n_l = pl.num_programs(1)
    tile = x_ref.shape[1]
    d_a = lbl_ref.shape[-1]
    d_b = lng_ref.shape[-1]
    dk = d_a // n_heads

    @pl.when(li == 0)
    def _():
        st_scr[...] = jnp.zeros_like(st_scr)

    x3 = x_ref[...]
    mod = mod_ref[...]
    zq, zf, zi, zg, zu, zv = _project_in(x3, mod, npre_ref[...], win_ref, d_a, d_b)

    lb = _forget_lower_bound(lbl_ref[...], layer)
    f = lb + (1.0 - lb) * jax.nn.sigmoid(zf)
    logf = jnp.log(f)
    k = 1.0 - f
    u = jax.nn.gelu(zu)
    vv = _layernorm(jax.nn.gelu(zv), lng_ref[...], lnb_ref[...])
    gn = gn_ref[...]

    col = lax.broadcasted_iota(jnp.int32, (dk, ROW_BLOCK), 1)
    for blk in range(tile // ROW_BLOCK):
        rows = slice(blk * ROW_BLOCK, (blk + 1) * ROW_BLOCK)
        qd, kd, k_end, b_last, causal = _gla_block_terms(logf[rows], zq[rows], k[rows], GLA_CHUNK)
        dec0 = jnp.exp(b_last[GLA_CHUNK - 1:GLA_CHUNK])
        dec1 = jnp.exp(b_last[ROW_BLOCK - 1:ROW_BLOCK])
        for h in range(n_heads):
            cs = slice(h * dk, (h + 1) * dk)
            qdh = _bf(qd[:, cs])
            vh = zi[rows, cs]
            scores = jnp.where(causal, _dot_nt(qdh, _bf(kd[:, cs])), 0.0)
            o_intra = _dot(_bf(scores), _bf(vh))
            vt = vh.T
            keh = _bf(k_end[:, cs])
            ds0 = _dot(_bf(jnp.where(col < GLA_CHUNK, vt, 0.0)), keh)
            ds1 = _dot(_bf(jnp.where(col >= GLA_CHUNK, vt, 0.0)), keh)
            st0 = st_scr[h]
            o0 = _dot_nt(qdh[:GLA_CHUNK], _bf(st0))
            st1 = st0 * dec0[:, cs] + ds0
            o1 = _dot_nt(qdh[GLA_CHUNK:], _bf(st1))
            st_scr[h] = st1 * dec1[:, cs] + ds1
            o = o_intra + jnp.concatenate([o0, o1], axis=0)
            m_scr[rows, cs] = _bf(_head_out(o, zg[rows, cs], gn))
        _spatial_gate_block(u[rows], vv[rows], wmix_ref, bias_ref, m_scr, rows, d_a, n_groups)

    m = _dot(m_scr[...], wout_ref[...])
    g1 = mod[:, 2:3, :]
    y_ref[...] = x3 + g1 * _rms(m, npost_ref[...]).reshape(x3.shape)

    @pl.when(li == n_l - 1)
    def _():
        for h in range(n_heads):
            st_ref[0, h] = st_scr[h].T
        vrow_ref[0] = vv[tile - ROW_BLOCK:]


def _mixer_sample_kernel(x_ref, mod_ref, s0_ref, lbl_ref, npre_ref, npost_ref, win_ref, gn_ref,
                         lng_ref, lnb_ref, wmix_ref, bias_ref, wout_ref, y_ref, st_ref, vrow_ref,
                         m_scr, *, layer, n_heads, n_groups):
    nb, seq, _ = x_ref.shape
    d_a = lbl_ref.shape[-1]
    d_b = lng_ref.shape[-1]
    dk = d_a // n_heads

    x3 = x_ref[...]
    mod = mod_ref[...]
    zq, zf, zi, zg, zu, zv = _project_in(x3, mod, npre_ref[...], win_ref, d_a, d_b)

    lb = _forget_lower_bound(lbl_ref[...], layer)
    f = lb + (1.0 - lb) * jax.nn.sigmoid(zf)
    logf = jnp.log(f)
    k = 1.0 - f
    u = jax.nn.gelu(zu)
    vv = _layernorm(jax.nn.gelu(zv), lng_ref[...], lnb_ref[...])
    gn = gn_ref[...]

    qd, kd, k_end, b_last, causal = _gla_block_terms(logf, zq, k, seq)
    decay = jnp.exp(b_last)
    rseg = lax.broadcasted_iota(jnp.int32, (ROW_BLOCK, 1), 0) // seq
    cseg = lax.broadcasted_iota(jnp.int32, (1, ROW_BLOCK), 1) // seq
    rows = slice(0, ROW_BLOCK)
    for h in range(n_heads):
        cs = slice(h * dk, (h + 1) * dk)
        qh = qd[:, cs]
        vhb = _bf(zi[:, cs])
        scores = jnp.where(causal, _dot_nt(_bf(qh), _bf(kd[:, cs])), 0.0)
        o_intra = _dot(_bf(scores), vhb)
        s0 = s0_ref[:, h]
        q_big = jnp.concatenate([_bf(jnp.where(rseg == j, qh, 0.0)) for j in range(nb)], axis=1)
        o_inter = _dot(q_big, _bf(s0.reshape(nb * dk, dk)))
        ket = k_end[:, cs].T
        k_big = jnp.concatenate([_bf(jnp.where(cseg == j, ket, 0.0)) for j in range(nb)], axis=0)
        ds = _dot(k_big, vhb).reshape(nb, dk, dk)
        dect = decay[:, cs].T
        for j in range(nb):
            dcol = jnp.broadcast_to(dect[:, j * seq:j * seq + 1], (dk, dk))
            st_ref[j, h] = dcol * s0[j] + ds[j]
        m_scr[rows, cs] = _bf(_head_out(o_intra + o_inter, zg[:, cs], gn))
    _spatial_gate_block(u, vv, wmix_ref, bias_ref, m_scr, rows, d_a, n_groups)

    m = _dot(m_scr[...], wout_ref[...])
    g1 = mod[:, 2:3, :]
    y_ref[...] = x3 + g1 * _rms(m, npost_ref[...]).reshape(x3.shape)
    vrow_ref[...] = vv.reshape(nb, seq, d_b)


def _ffn_kernel(x_ref, mod_ref, npre_ref, npost_ref, wg_ref, wu_ref, wd_ref, y_ref):
    x3 = x_ref[...]
    mod = mod_ref[...]
    sh, sc, g2 = mod[:, 3:4, :], mod[:, 4:5, :], mod[:, 5:6, :]
    h3 = _rms(x3, npre_ref[...]) * (1.0 + sc) + sh
    hb = _bf(h3.reshape(x3.shape[0] * x3.shape[1], x3.shape[2]))
    a = _bf(jax.nn.silu(_dot(hb, wg_ref[...])) * _dot(hb, wu_ref[...]))
    f = _dot(a, wd_ref[...])
    y_ref[...] = x3 + g2 * _rms(f, npost_ref[...]).reshape(x3.shape)


def _ada_kernel(c_ref, w_ref, b_ref, o_ref):
    o_ref[...] = _dot(_bf(jax.nn.silu(c_ref[...])), _bf(w_ref[...])) + b_ref[...]


def _const_spec(shape):
    nd = len(shape)
    return pl.BlockSpec(shape, lambda *_: (0,) * nd, pipeline_mode=pl.Buffered(1))


def _params(n_grid):
    return pltpu.CompilerParams(dimension_semantics=("arbitrary",) * n_grid,
                                vmem_limit_bytes=V7X_VMEM_LIMIT_BYTES)


def _ada(c_all, w_ada, b_ada):
    n, d = c_all.shape
    n_out = w_ada.shape[1]
    tn = d
    return pl.pallas_call(
        _ada_kernel,
        grid=(n_out // tn,),
        in_specs=[pl.BlockSpec((n, d), lambda j: (0, 0)),
                  pl.BlockSpec((d, tn), lambda j: (0, j)),
                  pl.BlockSpec((1, tn), lambda j: (0, j))],
        out_specs=pl.BlockSpec((n, tn), lambda j: (0, j)),
        out_shape=jax.ShapeDtypeStruct((n, n_out), F32),
        compiler_params=_params(1),
        name="ada_mod",
    )(c_all, w_ada, b_ada.reshape(1, n_out))


def _mixer_prompt(x, mod, consts, *, layer, n_heads, n_groups, tile):
    b, l, d = x.shape
    lbl, npre, npost, win, gn, lng, lnb, wmix, bias, wout = consts
    d_a, d_b = lbl.shape[-1], lng.shape[-1]
    dk = d_a // n_heads
    assert l % tile == 0 and tile % ROW_BLOCK == 0 and ROW_BLOCK == 2 * GLA_CHUNK
    kern = functools.partial(_mixer_prompt_kernel, layer=layer, n_heads=n_heads, n_groups=n_groups)
    return pl.pallas_call(
        kern,
        grid=(b, l // tile),
        in_specs=[pl.BlockSpec((1, tile, d), lambda i, j: (i, j, 0)),
                  pl.BlockSpec((1, 6, d), lambda i, j: (i, 0, 0))]
                 + [_const_spec(c.shape) for c in consts],
        out_specs=[pl.BlockSpec((1, tile, d), lambda i, j: (i, j, 0)),
                   pl.BlockSpec((1, n_heads, dk, dk), lambda i, j: (i, 0, 0, 0)),
                   pl.BlockSpec((1, ROW_BLOCK, d_b), lambda i, j: (i, 0, 0))],
        out_shape=[jax.ShapeDtypeStruct((b, l, d), F32),
                   jax.ShapeDtypeStruct((b, n_heads, dk, dk), F32),
                   jax.ShapeDtypeStruct((b, ROW_BLOCK, d_b), F32)],
        scratch_shapes=[pltpu.VMEM((n_heads, dk, dk), F32), pltpu.VMEM((tile, d_a + d_b), BF16)],
        compiler_params=_params(2),
        name="mixer_prompt",
    )(x, mod, *consts)


def _mixer_sample(x, mod, s0, consts, *, layer, n_heads, n_groups):
    b, seq, d = x.shape
    lbl, npre, npost, win, gn, lng, lnb, wmix, bias, wout = consts
    d_a, d_b = lbl.shape[-1], lng.shape[-1]
    dk = d_a // n_heads
    assert ROW_BLOCK % seq == 0 and seq % 8 == 0
    nb = ROW_BLOCK // seq
    assert b % nb == 0
    kern = functools.partial(_mixer_sample_kernel, layer=layer, n_heads=n_heads, n_groups=n_groups)
    return pl.pallas_call(
        kern,
        grid=(b // nb,),
        in_specs=[pl.BlockSpec((nb, seq, d), lambda i: (i, 0, 0)),
                  pl.BlockSpec((nb, 6, d), lambda i: (i, 0, 0)),
                  pl.BlockSpec((nb, n_heads, dk, dk), lambda i: (i, 0, 0, 0))]
                 + [_const_spec(c.shape) for c in consts],
        out_specs=[pl.BlockSpec((nb, seq, d), lambda i: (i, 0, 0)),
                   pl.BlockSpec((nb, n_heads, dk, dk), lambda i: (i, 0, 0, 0)),
                   pl.BlockSpec((nb, seq, d_b), lambda i: (i, 0, 0))],
        out_shape=[jax.ShapeDtypeStruct((b, seq, d), F32),
                   jax.ShapeDtypeStruct((b, n_heads, dk, dk), F32),
                   jax.ShapeDtypeStruct((b, seq, d_b), F32)],
        scratch_shapes=[pltpu.VMEM((ROW_BLOCK, d_a + d_b), BF16)],
        compiler_params=_params(1),
        name="mixer_sample",
    )(x, mod, s0, *consts)


def _ffn(x, mod, consts, *, bt, lt, name):
    b, l, d = x.shape
    assert b % bt == 0 and l % lt == 0
    return pl.pallas_call(
        _ffn_kernel,
        grid=(b // bt, l // lt),
        in_specs=[pl.BlockSpec((bt, lt, d), lambda i, j: (i, j, 0)),
                  pl.BlockSpec((bt, 6, d), lambda i, j: (i, 0, 0))]
                 + [_const_spec(c.shape) for c in consts],
        out_specs=pl.BlockSpec((bt, lt, d), lambda i, j: (i, j, 0)),
        out_shape=jax.ShapeDtypeStruct((b, l, d), F32),
        compiler_params=_params(2),
        name=name,
    )(x, mod, *consts)


def kernel(x_prompt, x_sample, state_hgrn, c_prompt, c_sample, lb_logits, w_ada, b_ada, norm_pre_mix, norm_post_mix, w_in, gnorm_w, ln_v_g, ln_v_b, w_spatial, b_spatial, w_out, norm_pre_ffn, norm_post_ffn, w_gate, w_up, w_down):
    depth, n_dec, n_heads, dk, _ = state_hgrn.shape
    n_groups, chunk = w_spatial.shape[1], w_spatial.shape[2]
    n_prompt, seq_p, d = x_prompt.shape
    dec_seq = x_sample.shape[1]
    d_b = ln_v_g.shape[-1]
    dh = d_b // n_groups
    assert chunk == ROW_BLOCK and seq_p % chunk == 0 and dec_seq <= chunk

    c_all = jnp.concatenate([c_prompt, c_sample], axis=0)
    lbl = lb_logits.astype(F32)
    causal = jnp.tril(jnp.ones((chunk, chunk), dtype=bool))
    n_seq_blk = ROW_BLOCK // dec_seq

    xp, xs = x_prompt, x_sample
    s_p, s_s, v_p, v_s = [], [], [], []
    for l in range(depth):
        mod = _ada(c_all, w_ada[l], b_ada[l]).reshape(n_prompt + n_dec, 6, d)
        mod_p, mod_s = mod[:n_prompt], mod[n_prompt:]

        ws = jnp.where(causal, w_spatial[l], 0.0)
        bias_p = jnp.repeat(b_spatial[l].T, dh, axis=1)
        ws_s = jnp.einsum("ab,gts->gatbs", jnp.eye(n_seq_blk, dtype=F32),
                          ws[:, :dec_seq, :dec_seq]).reshape(n_groups, ROW_BLOCK, ROW_BLOCK)
        bias_s = jnp.tile(bias_p[:dec_seq], (n_seq_blk, 1))
        row = lambda a: a.reshape(1, -1)
        shared = (lbl, row(norm_pre_mix[l]), row(norm_post_mix[l]), _bf(w_in[l]), row(gnorm_w[l]),
                  row(ln_v_g[l]), row(ln_v_b[l]))
        consts_p = shared + (_bf(ws), bias_p, _bf(w_out[l]))
        consts_s = shared + (_bf(ws_s), bias_s, _bf(w_out[l]))
        ffn_consts = (row(norm_pre_ffn[l]), row(norm_post_ffn[l]), _bf(w_gate[l]), _bf(w_up[l]),
                      _bf(w_down[l]))

        xp, sp, vp = _mixer_prompt(xp, mod_p, consts_p, layer=l, n_heads=n_heads,
                                   n_groups=n_groups, tile=256)
        xs, ss, vs = _mixer_sample(xs, mod_s, state_hgrn[l], consts_s, layer=l, n_heads=n_heads,
                                   n_groups=n_groups)
        xp = _ffn(xp, mod_p, ffn_consts, bt=1, lt=512, name="ffn_prompt")
        xs = _ffn(xs, mod_s, ffn_consts, bt=min(n_dec, 512 // dec_seq), lt=dec_seq, name="ffn_sample")

        s_p.append(sp)
        s_s.append(ss)
        v_p.append(vp.reshape(n_prompt, ROW_BLOCK, n_groups, dh))
        v_s.append(vs.reshape(n_dec, dec_seq, n_groups, dh))
    return (xp, xs, jnp.stack(s_p), jnp.stack(s_s), jnp.stack(v_p), jnp.stack(v_s))
```
